```python
import jax, jax.numpy as jnp
from jax import lax
import numpy as np

D_MODEL = 1024
BATCH = 16
SEQ = 2048
DEPTH = 1

GRID_W = 64
N_HEADS = 8
N_KV_HEADS = 2
HEAD_DIM = 64
ATTN_WIDTH = N_HEADS * HEAD_DIM
KV_WIDTH = N_KV_HEADS * HEAD_DIM
CONV_WIDTH = D_MODEL // 2
CONV_KSIZE = 31
Q_BLOCK = 128
ROPE_THETA = 10000.0
HALF_ROT = HEAD_DIM // 2
N_GROUPS = 4
EXPERTS_PER_GROUP = 8
N_EXPERTS = N_GROUPS * EXPERTS_PER_GROUP
TOP_K = 2
D_EXPERT = D_MODEL // 2
MOE_BLOCK = 128
EPS = 1e-6
SPLITS = (2 * CONV_WIDTH,
          2 * CONV_WIDTH + ATTN_WIDTH,
          2 * CONV_WIDTH + ATTN_WIDTH + KV_WIDTH,
          2 * CONV_WIDTH + ATTN_WIDTH + 2 * KV_WIDTH,
          2 * CONV_WIDTH + ATTN_WIDTH + 2 * KV_WIDTH + D_MODEL)
IN_WIDTH = 2 * CONV_WIDTH + ATTN_WIDTH + 2 * KV_WIDTH + 2 * D_MODEL

kernel_name = "hybrid_conv_axialgqa_hmoe_block"


def rmsnorm(x, g):
    xf = x.astype(jnp.float32)
    y = xf * lax.rsqrt(jnp.mean(xf * xf, axis=-1, keepdims=True) + EPS) * g.astype(jnp.float32)
    return y.astype(x.dtype)


def layernorm(x, g, b):
    xf = x.astype(jnp.float32)
    mu = jnp.mean(xf, axis=-1, keepdims=True)
    var = jnp.mean(jnp.square(xf - mu), axis=-1, keepdims=True)
    y = (xf - mu) * lax.rsqrt(var + EPS) * g.astype(jnp.float32) + b.astype(jnp.float32)
    return y.astype(x.dtype)


def depthwise_conv(u, w, b):
    pad = CONV_KSIZE // 2
    out = lax.conv_general_dilated(
        u, w[:, None, :].astype(u.dtype), window_strides=(1,), padding=[(pad, pad)],
        dimension_numbers=('NWC', 'WIO', 'NWC'), feature_group_count=u.shape[-1])
    return out + b.astype(u.dtype)


def axial_rope_angles(seq_len):
    rows = seq_len // GRID_W
    row_idx = jnp.repeat(jnp.arange(rows, dtype=jnp.int32), GRID_W).astype(jnp.float32)
    col_idx = jnp.tile(jnp.arange(GRID_W, dtype=jnp.int32), rows).astype(jnp.float32)
    inv_freq = ROPE_THETA ** (-jnp.arange(0, HALF_ROT, 2, dtype=jnp.float32) / HALF_ROT)
    return row_idx[:, None] * inv_freq, col_idx[:, None] * inv_freq


def rotate(xs, ang):
    cos = jnp.cos(ang)[None, :, None, :]
    sin = jnp.sin(ang)[None, :, None, :]
    x1, x2 = jnp.split(xs.astype(jnp.float32), 2, axis=-1)
    return jnp.concatenate([x1 * cos - x2 * sin, x2 * cos + x1 * sin], axis=-1).astype(xs.dtype)


def apply_axial_rope(x, ang_row, ang_col):
    xr, xc = jnp.split(x, 2, axis=-1)
    return jnp.concatenate([rotate(xr, ang_row), rotate(xc, ang_col)], axis=-1)


def axial_gqa_attention(q, k, v):
    B, S = q.shape[:2]
    nb = S // Q_BLOCK
    G = N_HEADS // N_KV_HEADS
    qb = q.reshape(B, nb, Q_BLOCK, N_KV_HEADS, G, HEAD_DIM).transpose(1, 0, 3, 4, 2, 5)
    kt = k.transpose(0, 2, 1, 3)
    vt = v.transpose(0, 2, 1, 3)
    scale = HEAD_DIM ** -0.5

    def block(qblk):
        s = jnp.einsum('bkgqd,bksd->bkgqs', qblk, kt).astype(jnp.float32) * scale
        p = jax.nn.softmax(s, axis=-1)
        return jnp.einsum('bkgqs,bksd->bkgqd', p.astype(vt.dtype), vt)

    o = lax.map(block, qb)
    return o.transpose(1, 0, 4, 2, 3, 5).reshape(B, S, ATTN_WIDTH)


def hierarchical_moe(h, w_group, b_group, w_router, b_router, w_gate, w_up, w_down):
    B, S, D = h.shape
    N = B * S
    xt = h.reshape(N, D)
    grp_prob = jax.nn.softmax((xt @ w_group).astype(jnp.float32) + b_group.astype(jnp.float32), axis=-1)
    grp_p, grp_idx = lax.top_k(grp_prob, 1)
    exp_logits = ((xt @ w_router).astype(jnp.float32) + b_router.astype(jnp.float32)).reshape(
        N, N_GROUPS, EXPERTS_PER_GROUP)
    in_grp = jnp.take_along_axis(exp_logits, grp_idx[:, :, None], axis=1)[:, 0]
    top_p, top_local = lax.top_k(jax.nn.softmax(in_grp, axis=-1), TOP_K)
    top_p = top_p / jnp.sum(top_p, axis=-1, keepdims=True)
    weights = grp_p * top_p
    expert_idx = grp_idx * EXPERTS_PER_GROUP + top_local

    A = N * TOP_K
    e_flat = expert_idx.reshape(A)
    w_flat = weights.reshape(A)
    tok_flat = jnp.arange(A, dtype=jnp.int32) // TOP_K
    order = jnp.argsort(e_flat)
    e_sorted = e_flat[order]
    counts = jnp.zeros((N_EXPERTS,), jnp.int32).at[e_flat].add(1)
    padded = (counts + MOE_BLOCK - 1) // MOE_BLOCK * MOE_BLOCK
    starts = jnp.cumsum(counts) - counts
    pends = jnp.cumsum(padded)
    pstarts = pends - padded
    dest = pstarts[e_sorted] + (jnp.arange(A, dtype=jnp.int32) - starts[e_sorted])
    P = A + N_EXPERTS * MOE_BLOCK
    n_blocks = P // MOE_BLOCK
    slot_tok = jnp.full((P,), N, jnp.int32).at[dest].set(tok_flat[order])
    slot_w = jnp.zeros((P,), jnp.float32).at[dest].set(w_flat[order])
    block_expert = jnp.minimum(
        jnp.searchsorted(pends, jnp.arange(n_blocks, dtype=jnp.int32) * MOE_BLOCK, side='right'),
        N_EXPERTS - 1).astype(jnp.int32)
    x_pad = jnp.concatenate([xt, jnp.zeros((1, D), xt.dtype)], axis=0)
    xb = x_pad[slot_tok].reshape(n_blocks, MOE_BLOCK, D)

    def expert_block(args):
        xblk, e = args
        return (jax.nn.silu(xblk @ w_gate[e]) * (xblk @ w_up[e])) @ w_down[e]

    yb = lax.map(expert_block, (xb, block_expert)).reshape(P, D)
    yb = yb * slot_w[:, None].astype(yb.dtype)
    y = jax.ops.segment_sum(yb, slot_tok, num_segments=N + 1)[:N]
    return y.reshape(B, S, D)


def setup_inputs(seed: int = 0) -> dict:
    key = jax.random.key(seed)
    ks = jax.random.split(key, 32)
    f32 = jnp.float32
    L, D = DEPTH, D_MODEL

    def nrm(k, shape, scale):
        return jax.random.normal(k, shape, f32) * scale

    def gain(k, shape):
        return 1.0 + 0.05 * jax.random.normal(k, shape, f32)

    return {
        "x": jax.random.normal(ks[0], (BATCH, SEQ, D), f32),
        "c": jax.random.normal(ks[1], (BATCH, D), f32),
        "w_ada": nrm(ks[2], (L, D, 6 * D), 0.5 * D ** -0.5),
        "b_ada": nrm(ks[3], (L, 6 * D), 0.02),
        "g_pre_mix": gain(ks[4], (L, D)),
        "g_post_mix": gain(ks[5], (L, D)),
        "w_in": nrm(ks[6], (L, D, IN_WIDTH), D ** -0.5),
        "w_dw": nrm(ks[7], (L, CONV_KSIZE, CONV_WIDTH), CONV_KSIZE ** -0.5),
        "b_dw": nrm(ks[8], (L, CONV_WIDTH), 0.02),
        "conv_ln_g": gain(ks[9], (L, CONV_WIDTH)),
        "conv_ln_b": nrm(ks[10], (L, CONV_WIDTH), 0.02),
        "w_conv_out": nrm(ks[11], (L, CONV_WIDTH, D), CONV_WIDTH ** -0.5),
        "q_norm_g": gain(ks[12], (L, HEAD_DIM)),
        "k_norm_g": gain(ks[13], (L, HEAD_DIM)),
        "w_attn_out": nrm(ks[14], (L, ATTN_WIDTH, D), ATTN_WIDTH ** -0.5),
        "w_out": nrm(ks[15], (L, D, D), D ** -0.5),
        "g_pre_ffn": gain(ks[16], (L, D)),
        "g_post_ffn": gain(ks[17], (L, D)),
        "w_group": nrm(ks[18], (L, D, N_GROUPS), D ** -0.5),
        "b_group": nrm(ks[19], (L, N_GROUPS), 0.01),
        "w_router": nrm(ks[20], (L, D, N_EXPERTS), D ** -0.5),
        "b_router": nrm(ks[21], (L, N_EXPERTS), 0.01),
        "w_e_gate": nrm(ks[22], (L, N_EXPERTS, D, D_EXPERT), D ** -0.5),
        "w_e_up": nrm(ks[23], (L, N_EXPERTS, D, D_EXPERT), D ** -0.5),
        "w_e_down": nrm(ks[24], (L, N_EXPERTS, D_EXPERT, D), D_EXPERT ** -0.5),
    }


def reference(x, c, w_ada, b_ada, g_pre_mix, g_post_mix, w_in, w_dw, b_dw, conv_ln_g, conv_ln_b,
              w_conv_out, q_norm_g, k_norm_g, w_attn_out, w_out, g_pre_ffn, g_post_ffn,
              w_group, b_group, w_router, b_router, w_e_gate, w_e_up, w_e_down):
    B, S, D = x.shape
    ang_row, ang_col = axial_rope_angles(S)
    for l in range(DEPTH):
        ada = jax.nn.silu(c) @ w_ada[l] + b_ada[l]
        sh1, sc1, gt1, sh2, sc2, gt2 = jnp.split(ada[:, None, :], 6, axis=-1)

        h = rmsnorm(x, g_pre_mix[l]) * (1.0 + sc1) + sh1
        z = h @ w_in[l]
        u_glu, q, k, v, g_conv, g_attn = jnp.split(z, SPLITS, axis=-1)

        a, ag = jnp.split(u_glu, 2, axis=-1)
        u = depthwise_conv(a * jax.nn.sigmoid(ag), w_dw[l], b_dw[l])
        u = layernorm(u, conv_ln_g[l], conv_ln_b[l])
        conv_out = jax.nn.silu(u) @ w_conv_out[l]

        q = apply_axial_rope(rmsnorm(q.reshape(B, S, N_HEADS, HEAD_DIM), q_norm_g[l]), ang_row, ang_col)
        k = apply_axial_rope(rmsnorm(k.reshape(B, S, N_KV_HEADS, HEAD_DIM), k_norm_g[l]), ang_row, ang_col)
        v = v.reshape(B, S, N_KV_HEADS, HEAD_DIM)
        attn_out = axial_gqa_attention(q, k, v) @ w_attn_out[l]

        merged = jax.nn.sigmoid(g_conv) * conv_out + jax.nn.sigmoid(g_attn) * attn_out
        x = x + gt1 * rmsnorm(merged @ w_out[l], g_post_mix[l])

        h = rmsnorm(x, g_pre_ffn[l]) * (1.0 + sc2) + sh2
        y = hierarchical_moe(h, w_group[l], b_group[l], w_router[l], b_router[l],
                             w_e_gate[l], w_e_up[l], w_e_down[l])
        x = x + gt2 * rmsnorm(y, g_post_ffn[l])
    return x
```

```python
import functools

import jax
import jax.numpy as jnp
from jax import lax
from jax.experimental import pallas as pl
from jax.experimental.pallas import tpu as pltpu

F32 = jnp.float32
BF16 = jnp.bfloat16
I32 = jnp.int32

D_MODEL = 1024
GRID_W = 64
N_HEADS = 8
N_KV_HEADS = 2
HEAD_DIM = 64
ATTN_WIDTH = N_HEADS * HEAD_DIM
KV_WIDTH = N_KV_HEADS * HEAD_DIM
CONV_WIDTH = D_MODEL // 2
CONV_KSIZE = 31
ROPE_THETA = 10000.0
HALF_ROT = HEAD_DIM // 2
N_GROUPS = 4
EXPERTS_PER_GROUP = 8
N_EXPERTS = N_GROUPS * EXPERTS_PER_GROUP
D_EXPERT = D_MODEL // 2
EPS = 1e-6
IN_WIDTH = 2 * CONV_WIDTH + ATTN_WIDTH + 2 * KV_WIDTH + 2 * D_MODEL

LANES = 128
SUBLANES = 8
VMEM_LIMIT = 56 << 20

ROW_TILE = 256
CONV_CHUNK = 32
CONV_HALO = 16
MOE_BLOCK = 256
RANK_BLOCK = 256
DISPATCH_CHUNK = 2048
COMBINE_TILE = 512
ROUTE_BASE = N_GROUPS


def _sigmoid(x):
    return 1.0 / (1.0 + jnp.exp(-x))


def _params(**kw):
    return pltpu.CompilerParams(vmem_limit_bytes=VMEM_LIMIT, **kw)


def _ada_kernel(c_ref, w_ref, b_ref, o_ref):
    c = c_ref[...]
    cs = (c * _sigmoid(c)).astype(BF16)
    o_ref[...] = jnp.dot(cs, w_ref[...].astype(BF16), preferred_element_type=F32) + b_ref[...]


def _ada(c, w, b):
    bsz, d = c.shape
    width = w.shape[1]
    tn = d
    return pl.pallas_call(
        _ada_kernel,
        grid=(width // tn,),
        in_specs=[pl.BlockSpec((bsz, d), lambda j: (0, 0)),
                  pl.BlockSpec((d, tn), lambda j: (0, j)),
                  pl.BlockSpec((1, tn), lambda j: (0, j))],
        out_specs=pl.BlockSpec((bsz, tn), lambda j: (0, j)),
        out_shape=jax.ShapeDtypeStruct((bsz, width), F32),
        compiler_params=_params(),
    )(c, w, b)


def _inproj_kernel(x_ref, mod_ref, g_ref, w_ref, cos_ref, sin_ref, qg_ref, kg_ref, hm_ref,
                   glu_ref, q_ref, k_ref, v_ref, sgc_ref, sga_ref):
    x = x_ref[...]
    ms = jnp.mean(x * x, axis=-1, keepdims=True)
    sh = mod_ref[0, 0:1, :]
    sc = mod_ref[0, 1:2, :]
    h = (x * lax.rsqrt(ms + EPS) * g_ref[...]) * (1.0 + sc) + sh
    hb = h.astype(BF16)

    def mm(lo, hi):
        return jnp.dot(hb, w_ref[:, lo:hi], preferred_element_type=F32)

    c0 = CONV_WIDTH
    glu_ref[...] = (mm(0, c0) * _sigmoid(mm(c0, 2 * c0))).astype(BF16)

    cos = cos_ref[...]
    sin = sin_ref[...]
    lane = lax.broadcasted_iota(I32, cos.shape, 1)
    first_half = (lane % HALF_ROT) < (HALF_ROT // 2)

    def norm_rope(z, g, scale):
        ms_h = jnp.dot(z * z, hm_ref[...], preferred_element_type=F32)
        zn = z * lax.rsqrt(ms_h + EPS) * g
        partner = jnp.where(first_half,
                            pltpu.roll(zn, LANES - HALF_ROT // 2, 1),
                            pltpu.roll(zn, HALF_ROT // 2, 1))
        return (zn * cos + partner * sin) * scale

    q0 = 2 * c0
    for j in range(ATTN_WIDTH // LANES):
        z = mm(q0 + j * LANES, q0 + (j + 1) * LANES)
        q_ref[:, j * LANES:(j + 1) * LANES] = norm_rope(z, qg_ref[...], HEAD_DIM ** -0.5).astype(BF16)
    k0 = q0 + ATTN_WIDTH
    k_ref[...] = norm_rope(mm(k0, k0 + KV_WIDTH), kg_ref[...], 1.0).astype(BF16)
    v0 = k0 + KV_WIDTH
    v_ref[...] = mm(v0, v0 + KV_WIDTH).astype(BF16)
    g0 = v0 + KV_WIDTH
    sgc_ref[...] = _sigmoid(mm(g0, g0 + D_MODEL)).astype(BF16)
    sga_ref[...] = _sigmoid(mm(g0 + D_MODEL, g0 + 2 * D_MODEL)).astype(BF16)


def _inproj(x2, mod, g_pre, w_in, cos_t, sin_t, qg, kg, head_mean, seq):
    n, d = x2.shape
    tm = ROW_TILE
    spb = seq // tm
    row = lambda i: (i, 0)
    const = lambda i: (0, 0)
    outs = [(CONV_WIDTH, BF16), (ATTN_WIDTH, BF16), (KV_WIDTH, BF16), (KV_WIDTH, BF16),
            (D_MODEL, BF16), (D_MODEL, BF16)]
    return pl.pallas_call(
        _inproj_kernel,
        grid=(n // tm,),
        in_specs=[pl.BlockSpec((tm, d), row),
                  pl.BlockSpec((1, 6, d), lambda i: (i // spb, 0, 0)),
                  pl.BlockSpec((1, d), const),
                  pl.BlockSpec((d, IN_WIDTH), const),
                  pl.BlockSpec((tm, LANES), lambda i: (i % spb, 0)),
                  pl.BlockSpec((tm, LANES), lambda i: (i % spb, 0)),
                  pl.BlockSpec((1, LANES), const),
                  pl.BlockSpec((1, LANES), const),
                  pl.BlockSpec((LANES, LANES), const)],
        out_specs=[pl.BlockSpec((tm, w), row) for w, _ in outs],
        out_shape=[jax.ShapeDtypeStruct((n, w), dt) for w, dt in outs],
        compiler_params=_params(),
    )(x2, mod, g_pre, w_in, cos_t, sin_t, qg, kg, head_mean)


def _conv_kernel(glu_ref, w_ref, b_ref, lg_ref, lb_ref, o_ref, pad_ref):
    seq = glu_ref.shape[0]
    zeros = jnp.zeros((CONV_HALO, CONV_WIDTH), F32)
    pad_ref[0:CONV_HALO, :] = zeros
    pad_ref[CONV_HALO + seq:2 * CONV_HALO + seq, :] = zeros
    pad_ref[CONV_HALO:CONV_HALO + seq, :] = glu_ref[...].astype(F32)
    shift = CONV_HALO - CONV_KSIZE // 2
    sub = SUBLANES

    def chunk(i, carry):
        base = pl.multiple_of(i * CONV_CHUNK, CONV_CHUNK)
        acc = jnp.zeros((CONV_CHUNK, CONV_WIDTH), F32) + b_ref[...]
        for r in range(sub):
            part = None
            for k in range(CONV_KSIZE):
                if (k + shift) % sub != r:
                    continue
                a8 = (k + shift) // sub * sub
                term = pad_ref[pl.ds(base + a8, CONV_CHUNK + sub), :] * w_ref[k:k + 1, :]
                part = term if part is None else part + term
            acc = acc + part[r:r + CONV_CHUNK, :]
        mu = jnp.mean(acc, axis=-1, keepdims=True)
        cen = acc - mu
        var = jnp.mean(cen * cen, axis=-1, keepdims=True)
        y = cen * lax.rsqrt(var + EPS) * lg_ref[...] + lb_ref[...]
        o_ref[pl.ds(base, CONV_CHUNK), :] = (y * _sigmoid(y)).astype(BF16)
        return carry

    lax.fori_loop(0, seq // CONV_CHUNK, chunk, 0)


def _conv(glu, w_dw, b_dw, ln_g, ln_b, seq):
    n, cw = glu.shape
    const = lambda b: (0, 0)
    return pl.pallas_call(
        _conv_kernel,
        grid=(n // seq,),
        in_specs=[pl.BlockSpec((seq, cw), lambda b: (b, 0)),
                  pl.BlockSpec((CONV_KSIZE, cw), const),
                  pl.BlockSpec((1, cw), const),
                  pl.BlockSpec((1, cw), const),
                  pl.BlockSpec((1, cw), const)],
        out_specs=pl.BlockSpec((seq, cw), lambda b: (b, 0)),
        out_shape=jax.ShapeDtypeStruct((n, cw), BF16),
        scratch_shapes=[pltpu.VMEM((seq + 2 * CONV_HALO, cw), F32)],
        compiler_params=_params(),
    )(glu, w_dw, b_dw, ln_g, ln_b)


def _attn_kernel(q_ref, k_ref, v_ref, o_ref):
    group = N_HEADS // N_KV_HEADS
    outs = []
    for h in range(N_HEADS):
        j = h // group
        qh = q_ref[:, h * HEAD_DIM:(h + 1) * HEAD_DIM]
        kj = k_ref[:, j * HEAD_DIM:(j + 1) * HEAD_DIM]
        vj = v_ref[:, j * HEAD_DIM:(j + 1) * HEAD_DIM]
        s = lax.dot_general(qh, kj, (((1,), (1,)), ((), ())), preferred_element_type=F32)
        m = jnp.max(s, axis=-1, keepdims=True)
        p = jnp.exp(s - m)
        l = jnp.sum(p, axis=-1, keepdims=True)
        o = jnp.dot(p.astype(BF16), vj, preferred_element_type=F32)
        outs.append(o / l)
    o_ref[...] = jnp.concatenate(outs, axis=-1).astype(BF16)


def _attention(q, k, v, seq):
    n = q.shape[0]
    tq = ROW_TILE
    spb = seq // tq
    return pl.pallas_call(
        _attn_kernel,
        grid=(n // tq,),
        in_specs=[pl.BlockSpec((tq, ATTN_WIDTH), lambda i: (i, 0)),
                  pl.BlockSpec((seq, KV_WIDTH), lambda i: (i // spb, 0)),
                  pl.BlockSpec((seq, KV_WIDTH), lambda i: (i // spb, 0))],
        out_specs=pl.BlockSpec((tq, ATTN_WIDTH), lambda i: (i, 0)),
        out_shape=jax.ShapeDtypeStruct((n, ATTN_WIDTH), BF16),
        compiler_params=_params(),
    )(q, k, v)


def _postmix_kernel(uc_ref, ao_ref, sgc_ref, sga_ref, x_ref, mod_ref, wc_ref, wa_ref, wo_ref,
                    gpost_ref, gpre_ref, wr_hi_ref, wr_lo_ref, br_ref,
                    x1_ref, h2_ref, route_ref):
    conv_out = jnp.dot(uc_ref[...], wc_ref[...], preferred_element_type=F32)
    attn_out = jnp.dot(ao_ref[...], wa_ref[...], preferred_element_type=F32)
    merged = sgc_ref[...].astype(F32) * conv_out + sga_ref[...].astype(F32) * attn_out
    m2 = jnp.dot(merged.astype(BF16), wo_ref[...], preferred_element_type=F32)
    ms = jnp.mean(m2 * m2, axis=-1, keepdims=True)
    gt1 = mod_ref[0, 2:3, :]
    x1 = x_ref[...] + gt1 * (m2 * lax.rsqrt(ms + EPS) * gpost_ref[...])
    x1_ref[...] = x1

    sh2 = mod_ref[0, 3:4, :]
    sc2 = mod_ref[0, 4:5, :]
    ms1 = jnp.mean(x1 * x1, axis=-1, keepdims=True)
    h2 = (x1 * lax.rsqrt(ms1 + EPS) * gpre_ref[...]) * (1.0 + sc2) + sh2
    h2_ref[...] = h2

    h_hi = h2.astype(BF16)
    h_lo = (h2 - h_hi.astype(F32)).astype(BF16)
    logits = (jnp.dot(h_hi, wr_hi_ref[...], preferred_element_type=F32)
              + jnp.dot(h_lo, wr_hi_ref[...], preferred_element_type=F32)
              + jnp.dot(h_hi, wr_lo_ref[...], preferred_element_type=F32)) + br_ref[...]

    lane = lax.broadcasted_iota(I32, logits.shape, 1)
    neg = jnp.float32(-jnp.inf)
    big = jnp.int32(LANES)

    def top1(vals):
        m = jnp.max(vals, axis=-1, keepdims=True)
        idx = jnp.min(jnp.where(vals == m, lane, big), axis=-1, keepdims=True)
        return m, idx

    gl = jnp.where(lane < N_GROUPS, logits, neg)
    gmax, gidx = top1(gl)
    gsum = jnp.sum(jnp.exp(gl - gmax), axis=-1, keepdims=True)
    grp_p = 1.0 / gsum
    lo = ROUTE_BASE + gidx * EXPERTS_PER_GROUP
    el = jnp.where((lane >= lo) & (lane < lo + EXPERTS_PER_GROUP), logits, neg)
    m1, i1 = top1(el)
    m2e, i2 = top1(jnp.where(lane == i1, neg, el))
    t = jnp.exp(m2e - m1)
    w1 = grp_p / (1.0 + t)
    w2 = grp_p * t / (1.0 + t)
    e1 = (i1 - ROUTE_BASE).astype(F32)
    e2 = (i2 - ROUTE_BASE).astype(F32)
    route_ref[...] = jnp.where(lane == 0, e1,
                               jnp.where(lane == 1, e2,
                                         jnp.where(lane == 2, w1,
                                                   jnp.where(lane == 3, w2, 0.0))))


def _postmix(uc, ao, sgc, sga, x2, mod, wc, wa, wo, gpost, gpre, wr_hi, wr_lo, br, seq):
    n, d = x2.shape
    tm = ROW_TILE
    spb = seq // tm
    row = lambda i: (i, 0)
    const = lambda i: (0, 0)
    return pl.pallas_call(
        _postmix_kernel,
        grid=(n // tm,),
        in_specs=[pl.BlockSpec((tm, CONV_WIDTH), row),
                  pl.BlockSpec((tm, ATTN_WIDTH), row),
                  pl.BlockSpec((tm, d), row),
                  pl.BlockSpec((tm, d), row),
                  pl.BlockSpec((tm, d), row),
                  pl.BlockSpec((1, 6, d), lambda i: (i // spb, 0, 0)),
                  pl.BlockSpec((CONV_WIDTH, d), const),
                  pl.BlockSpec((ATTN_WIDTH, d), const),
                  pl.BlockSpec((d, d), const),
                  pl.BlockSpec((1, d), const),
                  pl.BlockSpec((1, d), const),
                  pl.BlockSpec((d, LANES), const),
                  pl.BlockSpec((d, LANES), const),
                  pl.BlockSpec((1, LANES), const)],
        out_specs=[pl.BlockSpec((tm, d), row), pl.BlockSpec((tm, d), row), pl.BlockSpec((tm, LANES), row)],
        out_shape=[jax.ShapeDtypeStruct((n, d), F32), jax.ShapeDtypeStruct((n, d), F32),
                   jax.ShapeDtypeStruct((n, LANES), F32)],
        compiler_params=_params(),
    )(uc, ao, sgc, sga, x2, mod, wc, wa, wo, gpost, gpre, wr_hi, wr_lo, br)


def _rank_kernel(e_ref, dest_ref, emeta_ref, be_ref, nused_ref, u_ref):
    nb = e_ref.shape[0]
    rb = RANK_BLOCK
    sub = lax.broadcasted_iota(I32, (N_EXPERTS, rb), 0)
    r_i = lax.broadcasted_iota(I32, (rb, rb), 0)
    c_i = lax.broadcasted_iota(I32, (rb, rb), 1)
    u_ref[...] = (r_i < c_i).astype(BF16)

    def count(i, acc):
        return acc + (sub == e_ref[pl.ds(i, 1), :]).astype(F32)

    acc = lax.fori_loop(0, nb, count, jnp.zeros((N_EXPERTS, rb), F32))
    cnt = jnp.sum(acc, axis=1, keepdims=True)
    nblk_e = jnp.floor((cnt + (MOE_BLOCK - 1)) * (1.0 / MOE_BLOCK))

    hi = jnp.floor(nblk_e * (1.0 / 16.0))
    lo_d = nblk_e - 16.0 * hi
    pad_rows = jnp.zeros((LANES - N_EXPERTS, LANES), F32)
    hi_m = jnp.concatenate([jnp.broadcast_to(hi, (N_EXPERTS, LANES)), pad_rows], axis=0)
    lo_m = jnp.concatenate([jnp.broadcast_to(lo_d, (N_EXPERTS, LANES)), pad_rows], axis=0)
    l_r = lax.broadcasted_iota(I32, (LANES, LANES), 0)
    l_c = lax.broadcasted_iota(I32, (LANES, LANES), 1)
    tri = (l_c < l_r).astype(F32)
    bstart = (16.0 * jnp.dot(tri, hi_m, preferred_element_type=F32)
              + jnp.dot(tri, lo_m, preferred_element_type=F32))[0:N_EXPERTS, 0:1]
    pstart = bstart * MOE_BLOCK

    def rank(i, carry):
        onehot = sub == e_ref[pl.ds(i, 1), :]
        oh = onehot.astype(F32)
        cum = jnp.dot(oh.astype(BF16), u_ref[...], preferred_element_type=F32)
        val = cum + (carry + pstart)
        d = jnp.sum(jnp.where(onehot, val, 0.0), axis=0, keepdims=True)
        dest_ref[pl.ds(i, 1), :] = d.astype(I32)
        return carry + jnp.sum(oh, axis=1, keepdims=True)

    lax.fori_loop(0, nb, rank, jnp.zeros((N_EXPERTS, 1), F32))

    lane_e = lax.broadcasted_iota(I32, (N_EXPERTS, LANES), 1)
    emeta = jnp.where(lane_e == 0, cnt, jnp.where(lane_e == 1, pstart,
                                                  jnp.where(lane_e == 2, nblk_e * MOE_BLOCK, 0.0)))
    emeta_ref[...] = emeta.astype(I32)
    bend = bstart + nblk_e
    blk = lax.broadcasted_iota(I32, (N_EXPERTS, be_ref.shape[1]), 1).astype(F32)
    be = jnp.sum((bend <= blk).astype(F32), axis=0, keepdims=True)
    be_ref[...] = jnp.minimum(be, N_EXPERTS - 1.0).astype(I32)
    nused_ref[...] = jnp.broadcast_to(jnp.sum(nblk_e, axis=0, keepdims=True), (1, LANES)).astype(I32)


def _rank(e2d, nblk_pad):
    nb = e2d.shape[0]
    return pl.pallas_call(
        _rank_kernel,
        out_shape=[jax.ShapeDtypeStruct((nb, RANK_BLOCK), I32),
                   jax.ShapeDtypeStruct((N_EXPERTS, LANES), I32),
                   jax.ShapeDtypeStruct((1, nblk_pad), I32),
                   jax.ShapeDtypeStruct((1, LANES), I32)],
        scratch_shapes=[pltpu.VMEM((RANK_BLOCK, RANK_BLOCK), BF16)],
        compiler_params=_params(),
    )(e2d)


def _dispatch_kernel(emeta_ref, dest_ref, h_ref, xb_ref, zero_ref, sem, pad_sem):
    step = pl.program_id(0)
    per = DISPATCH_CHUNK // 2
    tok0 = step * per

    def row_copy(j, k):
        return pltpu.make_async_copy(h_ref.at[pl.ds(tok0 + j, 1)],
                                     xb_ref.at[pl.ds(dest_ref[2 * j + k], 1)], sem)

    def issue(j, carry):
        row_copy(j, 0).start()
        row_copy(j, 1).start()
        return carry

    def drain(j, carry):
        row_copy(j, 0).wait()
        row_copy(j, 1).wait()
        return carry

    lax.fori_loop(0, per, issue, 0)

    @pl.when(step == 0)
    def _():
        zero_ref[...] = jnp.zeros(zero_ref.shape, F32)
        total = xb_ref.shape[0]

        def zero_copy(r):
            return pltpu.make_async_copy(zero_ref.at[pl.ds(0, 1)], xb_ref.at[pl.ds(r, 1)], pad_sem)

        def fill(lo, hi):
            lax.fori_loop(lo, hi, lambda r, c: (zero_copy(r).start(), c)[1], 0)
            lax.fori_loop(lo, hi, lambda r, c: (zero_copy(r).wait(), c)[1], 0)

        def per_expert(e, carry):
            start = emeta_ref[3 * e + 1]
            fill(start + emeta_ref[3 * e], start + emeta_ref[3 * e + 2])
            return carry

        lax.fori_loop(0, N_EXPERTS, per_expert, 0)
        last = N_EXPERTS - 1
        fill(emeta_ref[3 * last + 1] + emeta_ref[3 * last + 2], total)

    lax.fori_loop(0, per, drain, 0)


def _dispatch(emeta_flat, dest_flat, h2, n_slots):
    a = dest_flat.shape[0]
    d = h2.shape[1]
    return pl.pallas_call(
        _dispatch_kernel,
        grid_spec=pltpu.PrefetchScalarGridSpec(
            num_scalar_prefetch=1,
            grid=(a // DISPATCH_CHUNK,),
            in_specs=[pl.BlockSpec((DISPATCH_CHUNK,), lambda i, m: (i,), memory_space=pltpu.SMEM),
                      pl.BlockSpec(memory_space=pl.ANY)],
            out_specs=pl.BlockSpec(memory_space=pl.ANY),
            scratch_shapes=[pltpu.VMEM((8, d), F32), pltpu.SemaphoreType.DMA, pltpu.SemaphoreType.DMA]),
        out_shape=jax.ShapeDtypeStruct((n_slots, d), F32),
        compiler_params=_params(),
    )(emeta_flat, dest_flat, h2)


def _experts_kernel(be_ref, nused_ref, x_ref, wg_ref, wu_ref, wd_ref, o_ref):
    i = pl.program_id(0)

    @pl.when(i < nused_ref[0])
    def _():
        x = x_ref[...].astype(BF16)
        g = jnp.dot(x, wg_ref[0].astype(BF16), preferred_element_type=F32)
        u = jnp.dot(x, wu_ref[0].astype(BF16), preferred_element_type=F32)
        hid = (g * _sigmoid(g) * u).astype(BF16)
        o_ref[...] = jnp.dot(hid, wd_ref[0].astype(BF16), preferred_element_type=F32)

    @pl.when(i >= nused_ref[0])
    def _():
        o_ref[...] = jnp.zeros(o_ref.shape, F32)


def _experts(be, nused, xb, wg, wu, wd):
    p, d = xb.shape
    tm = MOE_BLOCK
    used = lambda i, be, nu: jnp.minimum(i, nu[0] - 1)
    wmap = lambda i, be, nu: (be[used(i, be, nu)], 0, 0)
    return pl.pallas_call(
        _experts_kernel,
        grid_spec=pltpu.PrefetchScalarGridSpec(
            num_scalar_prefetch=2,
            grid=(p // tm,),
            in_specs=[pl.BlockSpec((tm, d), lambda i, be, nu: (used(i, be, nu), 0)),
                      pl.BlockSpec((1, d, D_EXPERT), wmap),
                      pl.BlockSpec((1, d, D_EXPERT), wmap),
                      pl.BlockSpec((1, D_EXPERT, d), wmap)],
            out_specs=pl.BlockSpec((tm, d), lambda i, be, nu: (i, 0))),
        out_shape=jax.ShapeDtypeStruct((p, d), F32),
        compiler_params=_params(),
    )(be, nused, xb, wg, wu, wd)


def _combine_kernel(dest_ref, yb_ref, route_ref, x1_ref, mod_ref, g_ref, o_ref, buf_ref, sem):
    t = COMBINE_TILE

    def row_copy(j):
        return pltpu.make_async_copy(yb_ref.at[pl.ds(dest_ref[j], 1)],
                                     buf_ref.at[pl.ds((j % 2) * t + j // 2, 1)], sem)

    lax.fori_loop(0, 2 * t, lambda j, c: (row_copy(j).start(), c)[1], 0)
    lax.fori_loop(0, 2 * t, lambda j, c: (row_copy(j).wait(), c)[1], 0)

    w1 = route_ref[:, 2:3]
    w2 = route_ref[:, 3:4]
    y = w1 * buf_ref[0:t, :] + w2 * buf_ref[t:2 * t, :]
    ms = jnp.mean(y * y, axis=-1, keepdims=True)
    gt2 = mod_ref[0, 5:6, :]
    o_ref[...] = x1_ref[...] + gt2 * (y * lax.rsqrt(ms + EPS) * g_ref[...])


def _combine(dest_flat, yb, route, x1, mod, g_post, seq):
    n, d = x1.shape
    t = COMBINE_TILE
    spb = seq // t
    row = lambda i: (i, 0)
    return pl.pallas_call(
        _combine_kernel,
        grid=(n // t,),
        in_specs=[pl.BlockSpec((2 * t,), lambda i: (i,), memory_space=pltpu.SMEM),
                  pl.BlockSpec(memory_space=pl.ANY),
                  pl.BlockSpec((t, LANES), row),
                  pl.BlockSpec((t, d), row),
                  pl.BlockSpec((1, 6, d), lambda i: (i // spb, 0, 0)),
                  pl.BlockSpec((1, d), lambda i: (0, 0))],
        out_specs=pl.BlockSpec((t, d), row),
        out_shape=jax.ShapeDtypeStruct((n, d), F32),
        scratch_shapes=[pltpu.VMEM((2 * t, d), F32), pltpu.SemaphoreType.DMA],
        compiler_params=_params(),
    )(dest_flat, yb, route, x1, mod, g_post)


def _rope_tables(seq):
    pos = jnp.arange(seq, dtype=I32)
    row = (pos // GRID_W).astype(F32)
    col = (pos % GRID_W).astype(F32)
    inv_freq = ROPE_THETA ** (-jnp.arange(0, HALF_ROT, 2, dtype=F32) / HALF_ROT)
    ar = row[:, None] * inv_freq
    ac = col[:, None] * inv_freq
    cos_h = jnp.concatenate([jnp.cos(ar), jnp.cos(ar), jnp.cos(ac), jnp.cos(ac)], axis=-1)
    sin_h = jnp.concatenate([-jnp.sin(ar), jnp.sin(ar), -jnp.sin(ac), jnp.sin(ac)], axis=-1)
    reps = LANES // HEAD_DIM
    return jnp.tile(cos_h, (1, reps)), jnp.tile(sin_h, (1, reps))


def _layer(x2, c, seq, w_ada, b_ada, g_pre_mix, g_post_mix, w_in, w_dw, b_dw, conv_ln_g, conv_ln_b,
           w_conv_out, q_norm_g, k_norm_g, w_attn_out, w_out, g_pre_ffn, g_post_ffn,
           w_group, b_group, w_router, b_router, w_e_gate, w_e_up, w_e_down, cos_t, sin_t):
    n, d = x2.shape
    bsz = c.shape[0]
    mod = _ada(c, w_ada, b_ada[None, :]).reshape(bsz, 6, d)

    reps = LANES // HEAD_DIM
    head_id = jnp.arange(LANES, dtype=I32) // HEAD_DIM
    head_mean = (head_id[:, None] == head_id[None, :]).astype(F32) / HEAD_DIM
    glu, q, k, v, sgc, sga = _inproj(
        x2, mod, g_pre_mix[None, :], w_in.astype(BF16), cos_t, sin_t,
        jnp.tile(q_norm_g, reps)[None, :], jnp.tile(k_norm_g, reps)[None, :], head_mean, seq)

    uc = _conv(glu, w_dw, b_dw[None, :], conv_ln_g[None, :], conv_ln_b[None, :], seq)
    ao = _attention(q, k, v, seq)

    w_route = jnp.concatenate(
        [w_group, w_router, jnp.zeros((d, LANES - N_GROUPS - N_EXPERTS), F32)], axis=1)
    wr_hi = w_route.astype(BF16)
    wr_lo = (w_route - wr_hi.astype(F32)).astype(BF16)
    b_route = jnp.concatenate(
        [b_group, b_router, jnp.zeros((LANES - N_GROUPS - N_EXPERTS,), F32)])[None, :]
    x1, h2, route = _postmix(uc, ao, sgc, sga, x2, mod, w_conv_out.astype(BF16), w_attn_out.astype(BF16),
                             w_out.astype(BF16), g_post_mix[None, :], g_pre_ffn[None, :],
                             wr_hi, wr_lo, b_route, seq)

    n_assign = 2 * n
    n_slots = n_assign + N_EXPERTS * MOE_BLOCK
    nblk = n_slots // MOE_BLOCK
    nblk_pad = -(-nblk // LANES) * LANES
    e2d = route[:, 0:2].astype(I32).reshape(n_assign // RANK_BLOCK, RANK_BLOCK)
    dest, emeta, be, nused = _rank(e2d, nblk_pad)
    dest_flat = dest.reshape(n_assign)
    xb = _dispatch(emeta[:, 0:3].reshape(3 * N_EXPERTS), dest_flat, h2, n_slots)
    yb = _experts(be[0, :nblk], nused[0, :1], xb, w_e_gate, w_e_up, w_e_down)
    return _combine(dest_flat, yb, route, x1, mod, g_post_ffn[None, :], seq)


def kernel(x, c, w_ada, b_ada, g_pre_mix, g_post_mix, w_in, w_dw, b_dw, conv_ln_g, conv_ln_b, w_conv_out,
           q_norm_g, k_norm_g, w_attn_out, w_out, g_pre_ffn, g_post_ffn, w_group, b_group, w_router,
           b_router, w_e_gate, w_e_up, w_e_down):
    bsz, seq, d = x.shape
    assert d == D_MODEL and seq % COMBINE_TILE == 0 and seq % GRID_W == 0
    assert (2 * bsz * seq) % DISPATCH_CHUNK == 0
    cos_t, sin_t = _rope_tables(seq)
    x2 = x.reshape(bsz * seq, d)
    for l in range(w_ada.shape[0]):
        x2 = _layer(x2, c, seq, w_ada[l], b_ada[l], g_pre_mix[l], g_post_mix[l], w_in[l], w_dw[l], b_dw[l],
                    conv_ln_g[l], conv_ln_b[l], w_conv_out[l], q_norm_g[l], k_norm_g[l], w_attn_out[l],
                    w_out[l], g_pre_ffn[l], g_post_ffn[l], w_group[l], b_group[l], w_router[l], b_router[l],
                    w_e_gate[l], w_e_up[l], w_e_down[l], cos_t, sin_t)
    return x2.reshape(bsz, seq, d)
```
